```python
import math, functools
import jax, jax.numpy as jnp
from jax import lax
import numpy as np

D_MODEL = 2048
BATCH = 8
SEQ = 2048
DEPTH = 2
DEC_BATCH = 128
DEC_SEQ = 4
PAST_LEN = 2048
PAGE_SIZE = 128

N_MIXERS = 2
MIX_WIDTH = D_MODEL
SB_HEADS = 16
SB_HEAD_DIM = MIX_WIDTH // SB_HEADS
SB_BIAS_INIT = -6.0
MEM_LEN = 256
MEM_HEADS = 4
MEM_HEAD_DIM = D_MODEL // 8
MEM_WIDTH = MEM_HEADS * MEM_HEAD_DIM
IN_WIDTH = 3 * MIX_WIDTH + MEM_WIDTH
OUT_WIDTH = MIX_WIDTH + MEM_WIDTH
CONV_W = 3
D_FF = 256 * ((8 * D_MODEL // 3 + 255) // 256)
Q_BLOCK = 128
EPS = 1e-6
N_CONV_LAYERS = (DEPTH + N_MIXERS - 1) // N_MIXERS
N_SB_LAYERS = DEPTH // N_MIXERS

kernel_name = 'interleaved_shortconv_stickbreaking_decoder_step'


def rmsnorm(x, g):
    xf = x.astype(jnp.float32)
    y = xf * lax.rsqrt(jnp.mean(xf * xf, axis=-1, keepdims=True) + EPS)
    return (y * g.astype(jnp.float32)).astype(x.dtype)


def causal_dwconv(prev, u, w):
    full = jnp.concatenate([prev.astype(u.dtype), u], axis=1)
    t = u.shape[1]
    y = full[:, 0:t] * w[0]
    for j in range(1, CONV_W):
        y = y + full[:, j:j + t] * w[j]
    return y, full[:, full.shape[1] - (CONV_W - 1):]


def conv_mixer(cols, prev, w_conv):
    b, c, h = jnp.split(cols, 3, axis=-1)
    y, new_prev = causal_dwconv(prev, c * h, w_conv)
    return b * y, new_prev


def stick_breaking_weights(z, mask):
    log_beta = jax.nn.log_sigmoid(z)
    log_keep = jnp.where(mask, jax.nn.log_sigmoid(-z), 0.0)
    after = lax.cumsum(log_keep, axis=z.ndim - 1, reverse=True) - log_keep
    return jnp.where(mask, jnp.exp(log_beta + after), 0.0)


def sb_attention_prompt(q, k, v, bias):
    b, s, h, dh = q.shape
    nb = s // Q_BLOCK
    qb = q.reshape(b, nb, Q_BLOCK, h, dh).transpose(1, 0, 2, 3, 4)
    key_pos = jnp.arange(s)
    scale = dh ** -0.5
    bias_f = bias.astype(jnp.float32)[None, :, None, None]

    def block(args):
        q_blk, bi = args
        z = jnp.einsum('bqhd,bshd->bhqs', q_blk, k, preferred_element_type=jnp.float32) * scale + bias_f
        q_pos = bi * Q_BLOCK + jnp.arange(Q_BLOCK)
        mask = key_pos[None, :] < q_pos[:, None]
        a = stick_breaking_weights(z, mask).astype(v.dtype)
        return jnp.einsum('bhqs,bshd->bqhd', a, v)

    o = lax.map(block, (qb, jnp.arange(nb)))
    return o.transpose(1, 0, 2, 3, 4).reshape(b, s, h * dh)


def sb_attention_sample(q, k, v, k_past, v_past, bias):
    bd, t, h, dh = q.shape
    p = k_past.shape[1]
    scale = dh ** -0.5
    bias_f = bias.astype(jnp.float32)[None, :, None, None]
    z_past = jnp.einsum('bqhd,bshd->bhqs', q, k_past, preferred_element_type=jnp.float32) * scale
    z_new = jnp.einsum('bqhd,bshd->bhqs', q, k, preferred_element_type=jnp.float32) * scale
    z = jnp.concatenate([z_past, z_new], axis=-1) + bias_f
    tq = jnp.arange(t)
    mask = jnp.concatenate([jnp.ones((t, p), bool), tq[None, :] < tq[:, None]], axis=-1)
    a = stick_breaking_weights(z, mask).astype(v.dtype)
    o = jnp.einsum('bhqs,bshd->bqhd', a[..., :p], v_past) + jnp.einsum('bhqs,bshd->bqhd', a[..., p:], v)
    return o.reshape(bd, t, h * dh)


def split_qkv(cols):
    b, t, _ = cols.shape
    q, k, v = jnp.split(cols, 3, axis=-1)
    shp = (b, t, SB_HEADS, SB_HEAD_DIM)
    return q.reshape(shp), k.reshape(shp), v.reshape(shp)


def sb_mixer_prompt(cols, bias):
    q, k, v = split_qkv(cols)
    return sb_attention_prompt(q, k, v, bias), (k, v)


def sb_mixer_sample(cols, k_past, v_past, bias):
    q, k, v = split_qkv(cols)
    return sb_attention_sample(q, k, v, k_past, v_past, bias), (k, v)


def memory_kv(mem, g_mem, w_mem_kv):
    b, m, _ = mem.shape
    kv = rmsnorm(mem, g_mem) @ w_mem_kv
    mk, mv = jnp.split(kv, 2, axis=-1)
    shp = (b, m, MEM_HEADS, MEM_HEAD_DIM)
    return mk.reshape(shp), mv.reshape(shp)


def memory_attention(q, mk, mv):
    b, t = q.shape[:2]
    z = jnp.einsum('bqhd,bkhd->bhqk', q, mk, preferred_element_type=jnp.float32) * (MEM_HEAD_DIM ** -0.5)
    p = jax.nn.softmax(z, axis=-1).astype(mv.dtype)
    return jnp.einsum('bhqk,bkhd->bqhd', p, mv).reshape(b, t, MEM_WIDTH)


def run_layer(x, mix_fn, mk, mv, ffn_prev, g_pre_mix, w_in, w_out, g_post_mix,
              g_pre_ffn, w_up, w_fconv, w_down, g_post_ffn):
    b, t, _ = x.shape
    h = rmsnorm(x, g_pre_mix) @ w_in
    y_mix, mix_state = mix_fn(h[..., :3 * MIX_WIDTH])
    q_mem = h[..., 3 * MIX_WIDTH:].reshape(b, t, MEM_HEADS, MEM_HEAD_DIM)
    y_mem = memory_attention(q_mem, mk, mv)
    y = jnp.concatenate([y_mix, y_mem], axis=-1) @ w_out
    x = x + rmsnorm(y, g_post_mix)
    hf = rmsnorm(x, g_pre_ffn) @ w_up
    g, u = jnp.split(hf, 2, axis=-1)
    gc, ffn_state = causal_dwconv(ffn_prev, g, w_fconv)
    f = (jax.nn.gelu(gc, approximate=True) * u) @ w_down
    x = x + rmsnorm(f, g_post_ffn)
    return x, mix_state, ffn_state


def setup_inputs(seed: int = 0) -> dict:
    key = jax.random.key(seed)
    ks = iter(jax.random.split(key, 32))

    def nrm(shape, scale):
        return jax.random.normal(next(ks), shape, jnp.float32) * scale

    def gain(shape):
        return 1.0 + nrm(shape, 0.05)

    n_pages = PAST_LEN // PAGE_SIZE
    used = DEC_BATCH * n_pages
    n_pool = used + max(1, used // 4)
    page_table = jax.random.permutation(next(ks), n_pool)[:used].reshape(DEC_BATCH, n_pages).astype(jnp.int32)
    return {
        'x_prompt': nrm((BATCH, SEQ, D_MODEL), 1.0),
        'x_sample': nrm((DEC_BATCH, DEC_SEQ, D_MODEL), 1.0),
        'cache_k': nrm((N_SB_LAYERS, n_pool, PAGE_SIZE, SB_HEADS, SB_HEAD_DIM), 1.0),
        'cache_v': nrm((N_SB_LAYERS, n_pool, PAGE_SIZE, SB_HEADS, SB_HEAD_DIM), 1.0),
        'cache_mem_k': nrm((DEPTH, DEC_BATCH, MEM_LEN, MEM_HEADS, MEM_HEAD_DIM), 1.0),
        'cache_mem_v': nrm((DEPTH, DEC_BATCH, MEM_LEN, MEM_HEADS, MEM_HEAD_DIM), 1.0),
        'state_mix_conv': nrm((N_CONV_LAYERS, DEC_BATCH, CONV_W - 1, MIX_WIDTH), 1.0),
        'state_ffn_conv': nrm((DEPTH, DEC_BATCH, CONV_W - 1, D_FF), 1.0),
        'page_table': page_table,
        'mem_prompt': nrm((BATCH, MEM_LEN, D_MODEL), 1.0),
        'g_pre_mix': gain((DEPTH, D_MODEL)),
        'w_in': nrm((DEPTH, D_MODEL, IN_WIDTH), D_MODEL ** -0.5),
        'w_mix_conv': nrm((N_CONV_LAYERS, CONV_W, MIX_WIDTH), CONV_W ** -0.5),
        'sb_bias': SB_BIAS_INIT + nrm((N_SB_LAYERS, SB_HEADS), 0.5),
        'w_out': nrm((DEPTH, OUT_WIDTH, D_MODEL), OUT_WIDTH ** -0.5),
        'g_post_mix': gain((DEPTH, D_MODEL)),
        'g_mem': gain((DEPTH, D_MODEL)),
        'w_mem_kv': nrm((DEPTH, D_MODEL, 2 * MEM_WIDTH), D_MODEL ** -0.5),
        'g_pre_ffn': gain((DEPTH, D_MODEL)),
        'w_ffn_up': nrm((DEPTH, D_MODEL, 2 * D_FF), D_MODEL ** -0.5),
        'w_ffn_conv': nrm((DEPTH, CONV_W, D_FF), CONV_W ** -0.5),
        'w_ffn_down': nrm((DEPTH, D_FF, D_MODEL), D_FF ** -0.5),
        'g_post_ffn': gain((DEPTH, D_MODEL)),
    }


def reference(x_prompt, x_sample, cache_k, cache_v, cache_mem_k, cache_mem_v, state_mix_conv,
              state_ffn_conv, page_table, mem_prompt, g_pre_mix, w_in, w_mix_conv, sb_bias, w_out,
              g_post_mix, g_mem, w_mem_kv, g_pre_ffn, w_ffn_up, w_ffn_conv, w_ffn_down, g_post_ffn):
    b_p = x_prompt.shape[0]
    b_s = x_sample.shape[0]
    past = page_table.shape[1] * cache_k.shape[2]
    xp, xs = x_prompt, x_sample
    k_p, v_p, k_s, v_s = [], [], [], []
    mk_p, mv_p = [], []
    cv_p, cv_s, fc_p, fc_s = [], [], [], []
    for l in range(DEPTH):
        j = l // N_MIXERS
        lw = (g_pre_mix[l], w_in[l], w_out[l], g_post_mix[l], g_pre_ffn[l],
              w_ffn_up[l], w_ffn_conv[l], w_ffn_down[l], g_post_ffn[l])
        mem_k, mem_v = memory_kv(mem_prompt, g_mem[l], w_mem_kv[l])
        mk_p.append(mem_k)
        mv_p.append(mem_v)
        if l % N_MIXERS == 0:
            zero_prev = jnp.zeros((b_p, CONV_W - 1, MIX_WIDTH), xp.dtype)
            mix_p = functools.partial(conv_mixer, prev=zero_prev, w_conv=w_mix_conv[j])
            mix_s = functools.partial(conv_mixer, prev=state_mix_conv[j], w_conv=w_mix_conv[j])
        else:
            k_past = cache_k[j][page_table].reshape(b_s, past, SB_HEADS, SB_HEAD_DIM)
            v_past = cache_v[j][page_table].reshape(b_s, past, SB_HEADS, SB_HEAD_DIM)
            mix_p = functools.partial(sb_mixer_prompt, bias=sb_bias[j])
            mix_s = functools.partial(sb_mixer_sample, k_past=k_past, v_past=v_past, bias=sb_bias[j])
        zero_ffn = jnp.zeros((b_p, CONV_W - 1, D_FF), xp.dtype)
        xp, st_p, f_p = run_layer(xp, mix_p, mem_k, mem_v, zero_ffn, *lw)
        xs, st_s, f_s = run_layer(xs, mix_s, cache_mem_k[l], cache_mem_v[l], state_ffn_conv[l], *lw)
        fc_p.append(f_p)
        fc_s.append(f_s)
        if l % N_MIXERS == 0:
            cv_p.append(st_p)
            cv_s.append(st_s)
        else:
            k_p.append(st_p[0])
            v_p.append(st_p[1])
            k_s.append(st_s[0])
            v_s.append(st_s[1])
    y_prompt = xp
    y_sample = xs
    new_k_prompt = jnp.stack(k_p)
    new_v_prompt = jnp.stack(v_p)
    new_k_sample = jnp.stack(k_s)
    new_v_sample = jnp.stack(v_s)
    new_mem_k_prompt = jnp.stack(mk_p)
    new_mem_v_prompt = jnp.stack(mv_p)
    new_mix_conv_prompt = jnp.stack(cv_p)
    new_mix_conv_sample = jnp.stack(cv_s)
    new_ffn_conv_prompt = jnp.stack(fc_p)
    new_ffn_conv_sample = jnp.stack(fc_s)
    return (y_prompt, y_sample, new_k_prompt, new_v_prompt, new_k_sample, new_v_sample,
            new_mem_k_prompt, new_mem_v_prompt, new_mix_conv_prompt, new_mix_conv_sample,
            new_ffn_conv_prompt, new_ffn_conv_sample)
```

```python
import functools

import jax
import jax.numpy as jnp
from jax import lax
from jax.experimental import pallas as pl
from jax.experimental.pallas import tpu as pltpu

F32 = jnp.float32
BF16 = jnp.bfloat16

D_MODEL = 2048
MIX_WIDTH = D_MODEL
SB_HEADS = 16
SB_HEAD_DIM = MIX_WIDTH // SB_HEADS
MEM_LEN = 256
MEM_HEADS = 4
MEM_HEAD_DIM = 256
MEM_WIDTH = MEM_HEADS * MEM_HEAD_DIM
IN_WIDTH = 3 * MIX_WIDTH + MEM_WIDTH
CONV_W = 3
EPS = 1e-6
N_MIXERS = 2

SUBLANES = 8
VMEM_LIMIT = 56 * 1024 * 1024


def _params(*sem):
    return pltpu.CompilerParams(dimension_semantics=sem, vmem_limit_bytes=VMEM_LIMIT)


def _rms(x, g):
    return x * lax.rsqrt(jnp.mean(x * x, axis=-1, keepdims=True) + EPS) * g


def _dot(a, b):
    return jnp.dot(a, b, preferred_element_type=F32)


def _dot_nt(a, b):
    return lax.dot_general(a, b, (((1,), (1,)), ((), ())), preferred_element_type=F32)


def _dot_tn(a, b):
    return lax.dot_general(a, b, (((0,), (0,)), ((), ())), preferred_element_type=F32)


def _rms_matmul_kernel(x_ref, g_ref, w_ref, o_ref, xn_ref):
    @pl.when(pl.program_id(1) == 0)
    def _():
        xn_ref[...] = _rms(x_ref[...], g_ref[...]).astype(BF16)

    o_ref[...] = _dot(xn_ref[...], w_ref[...])


def rms_matmul(x, g, w, *, tm=512, tn=1024):
    m, d = x.shape
    n = w.shape[1]
    return pl.pallas_call(
        _rms_matmul_kernel,
        grid=(m // tm, n // tn),
        in_specs=[
            pl.BlockSpec((tm, d), lambda i, j: (i, 0)),
            pl.BlockSpec((1, d), lambda i, j: (0, 0)),
            pl.BlockSpec((d, tn), lambda i, j: (0, j)),
        ],
        out_specs=pl.BlockSpec((tm, tn), lambda i, j: (i, j)),
        out_shape=jax.ShapeDtypeStruct((m, n), F32),
        scratch_shapes=[pltpu.VMEM((tm, d), BF16)],
        compiler_params=_params("parallel", "arbitrary"),
        name="rms_matmul",
    )(x, g.reshape(1, d), w)


def _conv3_halo(u, halo, w):
    rows = lax.broadcasted_iota(jnp.int32, halo.shape, 0)
    u1 = pltpu.roll(u, 1, 0)
    u2 = pltpu.roll(u, 2, 0)
    h1 = pltpu.roll(halo, 1, 0)
    h2 = pltpu.roll(halo, 2, 0)
    u1 = jnp.concatenate([jnp.where(rows < 1, h1, u1[:SUBLANES]), u1[SUBLANES:]], axis=0)
    u2 = jnp.concatenate([jnp.where(rows < 2, h2, u2[:SUBLANES]), u2[SUBLANES:]], axis=0)
    return u2 * w[0:1] + u1 * w[1:2] + u * w[2:3]


def _conv3_steps(u, p1, p2, w, steps):
    t = lax.broadcasted_iota(jnp.int32, u.shape, 0) & (steps - 1)
    u1 = jnp.where(t >= 1, pltpu.roll(u, 1, 0), p1)
    u2 = jnp.where(t >= 2, pltpu.roll(u, 2, 0), p2)
    return u2 * w[0:1] + u1 * w[1:2] + u * w[2:3]


def _prev_rows(prev, steps):
    b, _, c = prev.shape
    p2 = jnp.concatenate([prev, jnp.zeros((b, steps - 2, c), prev.dtype)], axis=1)
    p1 = jnp.concatenate([prev[:, 1:2], jnp.zeros((b, steps - 1, c), prev.dtype)], axis=1)
    return p1.reshape(b * steps, c), p2.reshape(b * steps, c)


CONV_CHUNK = 512


def _conv_mix_prompt_kernel(b_ref, c_ref, h_ref, hc_ref, hh_ref, w_ref, y_ref, st_ref, *, tps):
    first = pl.program_id(0) % tps == 0
    tm = b_ref.shape[0]
    for c0 in range(0, MIX_WIDTH, CONV_CHUNK):
        sl = slice(c0, c0 + CONV_CHUNK)
        u = c_ref[:, sl] * h_ref[:, sl]
        halo = jnp.where(first, 0.0, hc_ref[:, sl] * hh_ref[:, sl])
        y = b_ref[:, sl] * _conv3_halo(u, halo, w_ref[:, sl])
        y_ref[:, sl] = y.astype(y_ref.dtype)
        st_ref[:, sl] = u[tm - (CONV_W - 1):]


def conv_mix_prompt(h, w_conv, *, seq, tm=256):
    m = h.shape[0]
    tps = seq // tm
    nt = m // tm
    hb = tm // SUBLANES

    def halo_map(col):
        return lambda i: (jnp.maximum(i * hb - 1, 0), col)

    y, st = pl.pallas_call(
        functools.partial(_conv_mix_prompt_kernel, tps=tps),
        grid=(nt,),
        in_specs=[
            pl.BlockSpec((tm, MIX_WIDTH), lambda i: (i, 0)),
            pl.BlockSpec((tm, MIX_WIDTH), lambda i: (i, 1)),
            pl.BlockSpec((tm, MIX_WIDTH), lambda i: (i, 2)),
            pl.BlockSpec((SUBLANES, MIX_WIDTH), halo_map(1)),
            pl.BlockSpec((SUBLANES, MIX_WIDTH), halo_map(2)),
            pl.BlockSpec((CONV_W, MIX_WIDTH), lambda i: (0, 0)),
        ],
        out_specs=[
            pl.BlockSpec((tm, MIX_WIDTH), lambda i: (i, 0)),
            pl.BlockSpec((None, CONV_W - 1, MIX_WIDTH), lambda i: (i, 0, 0)),
        ],
        out_shape=[
            jax.ShapeDtypeStruct((m, MIX_WIDTH), BF16),
            jax.ShapeDtypeStruct((nt, CONV_W - 1, MIX_WIDTH), F32),
        ],
        compiler_params=_params("parallel"),
        name="conv_mix_prompt",
    )(h, h, h, h, h, w_conv)
    return y, st[tps - 1::tps]


def _conv_mix_sample_kernel(b_ref, c_ref, h_ref, p1_ref, p2_ref, w_ref, y_ref, u_ref, *, steps):
    for c0 in range(0, MIX_WIDTH, CONV_CHUNK):
        sl = slice(c0, c0 + CONV_CHUNK)
        u = c_ref[:, sl] * h_ref[:, sl]
        y = b_ref[:, sl] * _conv3_steps(u, p1_ref[:, sl], p2_ref[:, sl], w_ref[:, sl], steps)
        y_ref[:, sl] = y.astype(y_ref.dtype)
        u_ref[:, sl] = u


def conv_mix_sample(h, prev, w_conv, *, steps):
    m = h.shape[0]
    p1, p2 = _prev_rows(prev, steps)
    full = lambda col: pl.BlockSpec((m, MIX_WIDTH), lambda i: (0, col))
    y, u = pl.pallas_call(
        functools.partial(_conv_mix_sample_kernel, steps=steps),
        grid=(1,),
        in_specs=[full(0), full(1), full(2), full(0), full(0),
                  pl.BlockSpec((CONV_W, MIX_WIDTH), lambda i: (0, 0))],
        out_specs=[full(0), full(0)],
        out_shape=[jax.ShapeDtypeStruct((m, MIX_WIDTH), BF16),
                   jax.ShapeDtypeStruct((m, MIX_WIDTH), F32)],
        compiler_params=_params("arbitrary"),
        name="conv_mix_sample",
    )(h, h, h, p1, p2, w_conv)
    new_prev = u.reshape(m // steps, steps, MIX_WIDTH)[:, steps - (CONV_W - 1):]
    return y, new_prev


def _mem_attn_kernel(q_ref, mk_ref, mv_ref, o_ref):
    scale = MEM_HEAD_DIM ** -0.5
    for hd in range(MEM_HEADS):
        sl = slice(hd * MEM_HEAD_DIM, (hd + 1) * MEM_HEAD_DIM)
        q = (q_ref[:, sl] * scale).astype(BF16)
        z = _dot_nt(q, mk_ref[:, sl].astype(BF16))
        e = jnp.exp(z - jnp.max(z, axis=-1, keepdims=True))
        s = jnp.sum(e, axis=-1, keepdims=True)
        o = _dot(e.astype(BF16), mv_ref[:, sl].astype(BF16)) / s
        o_ref[:, sl] = o.astype(o_ref.dtype)


def mem_attn_prompt(h, memkv, *, seq, tm=512):
    m = h.shape[0]
    tps = seq // tm
    qcol = (IN_WIDTH - MEM_WIDTH) // MEM_WIDTH
    return pl.pallas_call(
        _mem_attn_kernel,
        grid=(m // tm,),
        in_specs=[
            pl.BlockSpec((tm, MEM_WIDTH), lambda i: (i, qcol)),
            pl.BlockSpec((MEM_LEN, MEM_WIDTH), lambda i: (i // tps, 0)),
            pl.BlockSpec((MEM_LEN, MEM_WIDTH), lambda i: (i // tps, 1)),
        ],
        out_specs=pl.BlockSpec((tm, MEM_WIDTH), lambda i: (i, 0)),
        out_shape=jax.ShapeDtypeStruct((m, MEM_WIDTH), BF16),
        compiler_params=_params("parallel"),
        name="mem_attn_prompt",
    )(h, memkv, memkv)


def mem_attn_sample(h3, mem_k, mem_v, layer):
    b, t, _ = h3.shape
    qcol = (IN_WIDTH - MEM_WIDTH) // MEM_WIDTH
    kv_spec = pl.BlockSpec((None, MEM_LEN, MEM_WIDTH), lambda i: (layer * b + i, 0, 0))
    return pl.pallas_call(
        _mem_attn_kernel,
        grid=(b,),
        in_specs=[pl.BlockSpec((None, t, MEM_WIDTH), lambda i: (i, 0, qcol)), kv_spec, kv_spec],
        out_specs=pl.BlockSpec((None, t, MEM_WIDTH), lambda i: (i, 0, 0)),
        out_shape=jax.ShapeDtypeStruct((b, t, MEM_WIDTH), F32),
        compiler_params=_params("parallel"),
        name="mem_attn_sample",
    )(h3, mem_k, mem_v)


def _log_sigmoids(z):
    t = jnp.log1p(jnp.exp(-jnp.abs(z)))
    return jnp.minimum(z, 0.0) - t, -(jnp.maximum(z, 0.0) + t)


def _split_bf16(x):
    hi = x.astype(BF16)
    lo = (x - hi.astype(F32)).astype(BF16)
    return hi, lo


def _suffix_triangle(n, keys_on_rows):
    j = jnp.arange(2 * n) % n
    s = jnp.arange(n)
    if keys_on_rows:
        return (j[None, :] > s[:, None]).astype(BF16)
    return (j[:, None] > s[None, :]).astype(BF16)


def _sb_prompt_kernel(q_ref, k_ref, v_ref, bias_ref, tri_ref, o_ref, k16_ref, v16_ref, *, tq):
    qi = pl.program_id(2)

    @pl.when(qi == 0)
    def _():
        k16_ref[...] = k_ref[...].astype(BF16)
        v16_ref[...] = v_ref[...].astype(BF16)

    q = (q_ref[...] * (SB_HEAD_DIM ** -0.5)).astype(BF16)
    bias = bias_ref[...]
    tri = tri_ref[...]

    def block(kb, carry, diag):
        run, acc = carry
        ks = pl.multiple_of(kb * tq, tq)
        z = _dot_nt(q, k16_ref[pl.ds(ks, tq), :]) + bias
        lb, lk = _log_sigmoids(z)
        if diag:
            mask = (lax.broadcasted_iota(jnp.int32, z.shape, 1)
                    < lax.broadcasted_iota(jnp.int32, z.shape, 0))
            lk = jnp.where(mask, lk, 0.0)
        hi, lo = _split_bf16(lk)
        after = _dot(jnp.concatenate([hi, lo], axis=1), tri) + run
        a = jnp.exp(lb + after)
        if diag:
            a = jnp.where(mask, a, 0.0)
        acc = acc + _dot(a.astype(BF16), v16_ref[pl.ds(ks, tq), :])
        run = run + jnp.sum(lk, axis=1, keepdims=True)
        return run, acc

    carry = (jnp.zeros((tq, 1), F32), jnp.zeros((tq, SB_HEAD_DIM), F32))
    carry = block(qi, carry, True)
    _, acc = lax.fori_loop(0, qi, lambda j, c: block(qi - 1 - j, c, False), carry)
    o_ref[...] = acc.astype(o_ref.dtype)


def sb_attn_prompt(h, bias, *, batch, seq, tq=256):
    m = h.shape[0]
    nq = seq // tq
    bias_b = jnp.broadcast_to(bias.astype(F32)[:, None, None], (SB_HEADS, 1, tq))
    tri = _suffix_triangle(tq, keys_on_rows=False)
    return pl.pallas_call(
        functools.partial(_sb_prompt_kernel, tq=tq),
        grid=(batch, SB_HEADS, nq),
        in_specs=[
            pl.BlockSpec((tq, SB_HEAD_DIM), lambda b, hd, qi: (b * nq + qi, hd)),
            pl.BlockSpec((seq, SB_HEAD_DIM), lambda b, hd, qi: (b, SB_HEADS + hd)),
            pl.BlockSpec((seq, SB_HEAD_DIM), lambda b, hd, qi: (b, 2 * SB_HEADS + hd)),
            pl.BlockSpec((None, 1, tq), lambda b, hd, qi: (hd, 0, 0)),
            pl.BlockSpec((2 * tq, tq), lambda b, hd, qi: (0, 0)),
        ],
        out_specs=pl.BlockSpec((tq, SB_HEAD_DIM), lambda b, hd, qi: (b * nq + qi, hd)),
        out_shape=jax.ShapeDtypeStruct((m, MIX_WIDTH), BF16),
        scratch_shapes=[pltpu.VMEM((seq, SB_HEAD_DIM), BF16), pltpu.VMEM((seq, SB_HEAD_DIM), BF16)],
        compiler_params=_params("parallel", "parallel", "arbitrary"),
        name="sb_attn_prompt",
    )(h, h, h, bias_b, tri)


NEW_PAD = 16


def _sb_sample_kernel(pt_ref, q_ref, kn_ref, vn_ref, kp_ref, vp_ref, bias_ref, trip_ref, trin_ref,
                      o_ref, wq_ref, acc_ref, run_ref, *, steps):
    del pt_ref
    p = pl.program_id(1)

    def head_match(shape):
        row_h = lax.broadcasted_iota(jnp.int32, shape, 0) & (SB_HEADS - 1)
        col_h = lax.broadcasted_iota(jnp.int32, shape, 1) // SB_HEAD_DIM
        return row_h == col_h

    def process(kk, vv, tri, masked):
        z = _dot_nt(kk, wq_ref[...]) + bias_ref[...]
        lb, lk = _log_sigmoids(z)
        if masked:
            mask = (lax.broadcasted_iota(jnp.int32, z.shape, 0)
                    < lax.broadcasted_iota(jnp.int32, z.shape, 1) // SB_HEADS)
            lk = jnp.where(mask, lk, 0.0)
        hi, lo = _split_bf16(lk)
        after = _dot(tri, jnp.concatenate([hi, lo], axis=0)) + run_ref[...]
        a = jnp.exp(lb + after)
        if masked:
            a = jnp.where(mask, a, 0.0)
        acc_ref[...] += _dot_tn(a.astype(BF16), vv)
        run_ref[...] += jnp.sum(lk, axis=0, keepdims=True)

    @pl.when(p == 0)
    def _():
        q = q_ref[...] * (SB_HEAD_DIM ** -0.5)
        qrep = jnp.concatenate(
            [jnp.broadcast_to(q[t:t + 1], (SB_HEADS, MIX_WIDTH)) for t in range(steps)], axis=0)
        wq_ref[...] = jnp.where(head_match(qrep.shape), qrep, 0.0).astype(BF16)
        acc_ref[...] = jnp.zeros_like(acc_ref)
        run_ref[...] = jnp.zeros_like(run_ref)
        pad = jnp.zeros((NEW_PAD - steps, MIX_WIDTH), F32)
        kn = jnp.concatenate([kn_ref[...], pad], axis=0).astype(BF16)
        vn = jnp.concatenate([vn_ref[...], pad], axis=0).astype(BF16)
        process(kn, vn, trin_ref[...], True)

    def heads_to_lanes(ref):
        n = ref.shape[0] // SB_HEADS
        return jnp.concatenate(
            [ref[pl.ds(hd, n, stride=SB_HEADS), :].astype(BF16) for hd in range(SB_HEADS)], axis=1)

    process(heads_to_lanes(kp_ref), heads_to_lanes(vp_ref), trip_ref[...], False)

    @pl.when(p == pl.num_programs(1) - 1)
    def _():
        acc = acc_ref[...]
        own = jnp.where(head_match(acc.shape), acc, 0.0)
        o_ref[...] = jnp.concatenate(
            [jnp.sum(own[t * SB_HEADS:(t + 1) * SB_HEADS], axis=0, keepdims=True)
             for t in range(steps)], axis=0)


def sb_attn_sample(h3, cache_k, cache_v, page_table, bias, page_base):
    b, t, _ = h3.shape
    n_pages = page_table.shape[1]
    page_rows = cache_k.shape[1]
    page = page_rows // SB_HEADS
    cols = t * SB_HEADS
    bias_row = jnp.tile(bias.astype(F32), t).reshape(1, cols)
    tri_p = _suffix_triangle(page, keys_on_rows=True)
    tri_n = _suffix_triangle(NEW_PAD, keys_on_rows=True)

    def page_map(i, p, pt):
        return (page_base + pt[i, n_pages - 1 - p], 0, 0)

    new_spec = lambda col: pl.BlockSpec((None, t, MIX_WIDTH), lambda i, p, pt: (i, 0, col))
    const = lambda shape: pl.BlockSpec(shape, lambda i, p, pt: (0, 0))
    grid_spec = pltpu.PrefetchScalarGridSpec(
        num_scalar_prefetch=1,
        grid=(b, n_pages),
        in_specs=[
            new_spec(0), new_spec(1), new_spec(2),
            pl.BlockSpec((None, page_rows, SB_HEAD_DIM), page_map),
            pl.BlockSpec((None, page_rows, SB_HEAD_DIM), page_map),
            const((1, cols)), const((page, 2 * page)), const((NEW_PAD, 2 * NEW_PAD)),
        ],
        out_specs=pl.BlockSpec((None, t, MIX_WIDTH), lambda i, p, pt: (i, 0, 0)),
        scratch_shapes=[
            pltpu.VMEM((cols, MIX_WIDTH), BF16),
            pltpu.VMEM((cols, MIX_WIDTH), F32),
            pltpu.VMEM((1, cols), F32),
        ],
    )
    return pl.pallas_call(
        functools.partial(_sb_sample_kernel, steps=t),
        grid_spec=grid_spec,
        out_shape=jax.ShapeDtypeStruct((b, t, MIX_WIDTH), F32),
        compiler_params=_params("parallel", "arbitrary"),
        name="sb_attn_sample",
    )(page_table, h3, h3, h3, cache_k, cache_v, bias_row, tri_p, tri_n)


def _out_proj_kernel(ymix_ref, ymem_ref, w1_ref, w2_ref, g_ref, x_ref, o_ref):
    y = _dot(ymix_ref[...].astype(BF16), w1_ref[...]) + _dot(ymem_ref[...].astype(BF16), w2_ref[...])
    o_ref[...] = x_ref[...] + _rms(y, g_ref[...])


def out_proj(ymix, ymem, w_out, g, x, *, tm=512):
    m, d = x.shape
    once = pl.Buffered(1)
    return pl.pallas_call(
        _out_proj_kernel,
        grid=(m // tm,),
        in_specs=[
            pl.BlockSpec((tm, MIX_WIDTH), lambda i: (i, 0)),
            pl.BlockSpec((tm, MEM_WIDTH), lambda i: (i, 0)),
            pl.BlockSpec((MIX_WIDTH, d), lambda i: (0, 0), pipeline_mode=once),
            pl.BlockSpec((MEM_WIDTH, d), lambda i: (MIX_WIDTH // MEM_WIDTH, 0), pipeline_mode=once),
            pl.BlockSpec((1, d), lambda i: (0, 0)),
            pl.BlockSpec((tm, d), lambda i: (i, 0)),
        ],
        out_specs=pl.BlockSpec((tm, d), lambda i: (i, 0)),
        out_shape=jax.ShapeDtypeStruct((m, d), F32),
        compiler_params=_params("parallel"),
        name="out_proj",
    )(ymix, ymem, w_out, w_out, g.reshape(1, d), x)


def _ffn_kernel(*refs, tps, steps):
    sample = steps is not None
    if sample:
        (x_ref, gpre_ref, wg_ref, wu_ref, wc_ref, wd_ref, gpost_ref, p1_ref, p2_ref,
         o_ref, st_ref, xn_ref, acc_ref) = refs
    else:
        (x_ref, gpre_ref, wg_ref, wu_ref, wc_ref, wd_ref, gpost_ref,
         o_ref, st_ref, xn_ref, acc_ref, carry_ref) = refs
    i = pl.program_id(0)
    k = pl.program_id(1)
    tm = x_ref.shape[0]

    @pl.when(k == 0)
    def _():
        xn_ref[...] = _rms(x_ref[...], gpre_ref[...]).astype(BF16)

    xn = xn_ref[...]
    g = _dot(xn, wg_ref[...])
    u = _dot(xn, wu_ref[...])
    if sample:
        gc = _conv3_steps(g, p1_ref[...], p2_ref[...], wc_ref[...], steps)
        st_ref[...] = g
    else:
        @pl.when(i % tps == 0)
        def _():
            carry_ref[k] = jnp.zeros(carry_ref.shape[1:], F32)

        gc = _conv3_halo(g, carry_ref[k], wc_ref[...])
        carry_ref[k] = g[tm - SUBLANES:]
        st_ref[...] = g[tm - (CONV_W - 1):]
    a = (jax.nn.gelu(gc, approximate=True) * u).astype(BF16)
    contrib = _dot(a, wd_ref[...])

    @pl.when(k == 0)
    def _():
        acc_ref[...] = contrib

    @pl.when(k > 0)
    def _():
        acc_ref[...] += contrib

    @pl.when(k == pl.num_programs(1) - 1)
    def _():
        o_ref[...] = x_ref[...] + _rms(acc_ref[...], gpost_ref[...])


def conv_ffn(x, g_pre, w_up, w_conv, w_down, g_post, *, seq=None, prev=None, steps=None, tm=512, tn=512):
    m, d = x.shape
    dff = w_down.shape[0]
    nk = dff // tn
    sample = steps is not None
    tps = None if sample else seq // tm
    nt = m // tm
    in_specs = [
        pl.BlockSpec((tm, d), lambda i, k: (i, 0)),
        pl.BlockSpec((1, d), lambda i, k: (0, 0)),
        pl.BlockSpec((d, tn), lambda i, k: (0, k)),
        pl.BlockSpec((d, tn), lambda i, k: (0, nk + k)),
        pl.BlockSpec((CONV_W, tn), lambda i, k: (0, k)),
        pl.BlockSpec((tn, d), lambda i, k: (k, 0)),
        pl.BlockSpec((1, d), lambda i, k: (0, 0)),
    ]
    args = [x, g_pre.reshape(1, d), w_up, w_up, w_conv, w_down, g_post.reshape(1, d)]
    scratch = [pltpu.VMEM((tm, d), BF16), pltpu.VMEM((tm, d), F32)]
    if sample:
        assert nt == 1
        p1, p2 = _prev_rows(prev, steps)
        in_specs += [pl.BlockSpec((tm, tn), lambda i, k: (0, k))] * 2
        args += [p1, p2]
        st_spec = pl.BlockSpec((tm, tn), lambda i, k: (0, k))
        st_shape = jax.ShapeDtypeStruct((m, dff), F32)
    else:
        scratch.append(pltpu.VMEM((nk, SUBLANES, tn), F32))
        st_spec = pl.BlockSpec((None, CONV_W - 1, tn), lambda i, k: (i, 0, k))
        st_shape = jax.ShapeDtypeStruct((nt, CONV_W - 1, dff), F32)
    y, st = pl.pallas_call(
        functools.partial(_ffn_kernel, tps=tps, steps=steps),
        grid=(nt, nk),
        in_specs=in_specs,
        out_specs=[pl.BlockSpec((tm, d), lambda i, k: (i, 0)), st_spec],
        out_shape=[jax.ShapeDtypeStruct((m, d), F32), st_shape],
        scratch_shapes=scratch,
        compiler_params=_params("arbitrary", "arbitrary"),
        name="conv_ffn_sample" if sample else "conv_ffn_prompt",
    )(*args)
    if sample:
        new_prev = st.reshape(m // steps, steps, dff)[:, steps - (CONV_W - 1):]
    else:
        new_prev = st[tps - 1::tps]
    return y, new_prev


def kernel(x_prompt, x_sample, cache_k, cache_v, cache_mem_k, cache_mem_v, state_mix_conv, state_ffn_conv, page_table, mem_prompt, g_pre_mix, w_in, w_mix_conv, sb_bias, w_out, g_post_mix, g_mem, w_mem_kv, g_pre_ffn, w_ffn_up, w_ffn_conv, w_ffn_down, g_post_ffn):
    bp, seq, d = x_prompt.shape
    bs, steps, _ = x_sample.shape
    depth = w_in.shape[0]
    n_sb, n_pool, page = cache_k.shape[:3]

    xp = x_prompt.reshape(bp * seq, d)
    xs = x_sample.reshape(bs * steps, d)
    mem = mem_prompt.reshape(bp * MEM_LEN, d)
    ck = cache_k.reshape(n_sb * n_pool, page * SB_HEADS, SB_HEAD_DIM)
    cv = cache_v.reshape(n_sb * n_pool, page * SB_HEADS, SB_HEAD_DIM)
    cmk = cache_mem_k.reshape(depth * bs, MEM_LEN, MEM_WIDTH)
    cmv = cache_mem_v.reshape(depth * bs, MEM_LEN, MEM_WIDTH)

    k_p, v_p, k_s, v_s, mk_p, mv_p = [], [], [], [], [], []
    cv_p, cv_s, fc_p, fc_s = [], [], [], []
    for l in range(depth):
        j = l // N_MIXERS
        wi = w_in[l].astype(BF16)
        wo = w_out[l].astype(BF16)
        wkv = w_mem_kv[l].astype(BF16)
        wup = w_ffn_up[l].astype(BF16)
        wdn = w_ffn_down[l].astype(BF16)

        memkv = rms_matmul(mem, g_mem[l], wkv)
        mk_p.append(memkv[:, :MEM_WIDTH].reshape(bp, MEM_LEN, MEM_HEADS, MEM_HEAD_DIM))
        mv_p.append(memkv[:, MEM_WIDTH:].reshape(bp, MEM_LEN, MEM_HEADS, MEM_HEAD_DIM))

        hp = rms_matmul(xp, g_pre_mix[l], wi)
        hs = rms_matmul(xs, g_pre_mix[l], wi)
        hs3 = hs.reshape(bs, steps, IN_WIDTH)
        if l % N_MIXERS == 0:
            ymix_p, st_p = conv_mix_prompt(hp, w_mix_conv[j], seq=seq)
            ymix_s, st_s = conv_mix_sample(hs, state_mix_conv[j], w_mix_conv[j], steps=steps)
            cv_p.append(st_p)
            cv_s.append(st_s)
        else:
            ymix_p = sb_attn_prompt(hp, sb_bias[j], batch=bp, seq=seq)
            ymix_s = sb_attn_sample(hs3, ck, cv, page_table, sb_bias[j], j * n_pool)
            ymix_s = ymix_s.reshape(bs * steps, MIX_WIDTH)
            shp = (bp, seq, SB_HEADS, SB_HEAD_DIM)
            k_p.append(hp[:, MIX_WIDTH:2 * MIX_WIDTH].reshape(shp))
            v_p.append(hp[:, 2 * MIX_WIDTH:3 * MIX_WIDTH].reshape(shp))
            shs = (bs, steps, SB_HEADS, SB_HEAD_DIM)
            k_s.append(hs[:, MIX_WIDTH:2 * MIX_WIDTH].reshape(shs))
            v_s.append(hs[:, 2 * MIX_WIDTH:3 * MIX_WIDTH].reshape(shs))
        ymem_p = mem_attn_prompt(hp, memkv, seq=seq)
        ymem_s = mem_attn_sample(hs3, cmk, cmv, l).reshape(bs * steps, MEM_WIDTH)

        xp = out_proj(ymix_p, ymem_p, wo, g_post_mix[l], xp)
        xs = out_proj(ymix_s, ymem_s, wo, g_post_mix[l], xs)

        xp, f_p = conv_ffn(xp, g_pre_ffn[l], wup, w_ffn_conv[l], wdn, g_post_ffn[l], seq=seq)
        xs, f_s = conv_ffn(xs, g_pre_ffn[l], wup, w_ffn_conv[l], wdn, g_post_ffn[l],
                           prev=state_ffn_conv[l], steps=steps)
        fc_p.append(f_p)
        fc_s.append(f_s)

    return (xp.reshape(bp, seq, d), xs.reshape(bs, steps, d),
            jnp.stack(k_p), jnp.stack(v_p), jnp.stack(k_s), jnp.stack(v_s),
            jnp.stack(mk_p), jnp.stack(mv_p), jnp.stack(cv_p), jnp.stack(cv_s),
            jnp.stack(fc_p), jnp.stack(fc_s))
```
